```python
import jax, jax.numpy as jnp
from jax import lax
import numpy as np

D_MODEL = 2048
BATCH = 4
SEQ = 4096
DEPTH = 1

HEAD_DIM = 128
N_Q_HEADS = 8
N_KV_HEADS = 2
Q_GROUP = N_Q_HEADS // N_KV_HEADS
ATTN_W = N_Q_HEADS * HEAD_DIM
KV_W = N_KV_HEADS * HEAD_DIM
Q_BLOCK = 128
ROPE_AXIS_DIM = HEAD_DIM // 2
ROPE_THETA = 10000.0
GRID_W = 64
CONV_W = D_MODEL // 2
CONV_K = 3
OFF_Q = 0
OFF_K = OFF_Q + ATTN_W
OFF_V = OFF_K + KV_W
OFF_CH = OFF_V + KV_W
OFF_CB = OFF_CH + CONV_W
OFF_CC = OFF_CB + CONV_W
OFF_GA = OFF_CC + CONV_W
OFF_GC = OFF_GA + D_MODEL
W_IN = OFF_GC + D_MODEL
PEER_HEADS = 8
PEER_N_KEYS = 128
PEER_N_EXPERTS = PEER_N_KEYS * PEER_N_KEYS
PEER_DK = 256
PEER_HALF = PEER_DK // 2
PEER_TOPK = 16
PEER_CHUNK = 128
PLE_DIM = 256
EPS = 1e-6

kernel_name = 'hybrid_gqa_shortconv_peer_encoder_block'


def rms_norm(x, g):
    xf = x.astype(jnp.float32)
    y = xf * lax.rsqrt(jnp.mean(xf * xf, axis=-1, keepdims=True) + EPS)
    return (y * g.astype(jnp.float32)).astype(x.dtype)


def axial_rope_tables(S):
    rows = S // GRID_W
    row = jnp.repeat(jnp.arange(rows, dtype=jnp.int32), GRID_W, total_repeat_length=S)
    col = jnp.tile(jnp.arange(GRID_W, dtype=jnp.int32), rows)
    inv = ROPE_THETA ** (-jnp.arange(0, ROPE_AXIS_DIM, 2, dtype=jnp.float32) / ROPE_AXIS_DIM)
    ang_r = row.astype(jnp.float32)[:, None] * inv[None, :]
    ang_c = col.astype(jnp.float32)[:, None] * inv[None, :]
    ang = jnp.concatenate([ang_r, ang_r, ang_c, ang_c], axis=-1)
    return jnp.cos(ang), jnp.sin(ang)


def apply_rope(x, cos, sin):
    xf = x.astype(jnp.float32)
    xs = xf.reshape(*x.shape[:-1], 2, 2, ROPE_AXIS_DIM // 2)
    rot = jnp.stack([-xs[..., 1, :], xs[..., 0, :]], axis=-2).reshape(x.shape)
    return (xf * cos[None, :, None, :] + rot * sin[None, :, None, :]).astype(x.dtype)


def block_attention(q, k, v):
    B, S = q.shape[0], q.shape[1]
    nblk = S // Q_BLOCK
    qb = q.reshape(B, nblk, Q_BLOCK, N_KV_HEADS, Q_GROUP, HEAD_DIM).transpose(1, 0, 3, 4, 2, 5)
    kt = k.transpose(0, 2, 1, 3)
    vt = v.transpose(0, 2, 1, 3)
    scale = HEAD_DIM ** -0.5

    def one_block(qblk):
        s = jnp.einsum('bkgqd,bksd->bkgqs', qblk, kt).astype(jnp.float32) * scale
        pr = jax.nn.softmax(s, axis=-1).astype(vt.dtype)
        return jnp.einsum('bkgqs,bksd->bkgqd', pr, vt)

    o = lax.map(one_block, qb)
    return o.transpose(1, 0, 4, 2, 3, 5).reshape(B, S, ATTN_W)


def short_conv(u, w):
    return lax.conv_general_dilated(
        u, w[:, None, :].astype(u.dtype), window_strides=(1,), padding=((1, 1),),
        dimension_numbers=('NWC', 'WIO', 'NWC'), feature_group_count=u.shape[-1])


def peer(h, w_q, sub_keys, u, v):
    B, S, D = h.shape
    T = B * S
    hc = h.reshape(T // PEER_CHUNK, PEER_CHUNK, D)

    def one_chunk(xc):
        q = (xc @ w_q).reshape(PEER_CHUNK, PEER_HEADS, 2, PEER_HALF)
        s = jnp.einsum('thpc,hpnc->thpn', q, sub_keys).astype(jnp.float32)
        sv, si = lax.top_k(s, PEER_TOPK)
        cand = (sv[:, :, 0, :, None] + sv[:, :, 1, None, :]).reshape(PEER_CHUNK, PEER_HEADS, PEER_TOPK * PEER_TOPK)
        cidx = (si[:, :, 0, :, None] * PEER_N_KEYS + si[:, :, 1, None, :]).reshape(PEER_CHUNK, PEER_HEADS, PEER_TOPK * PEER_TOPK)
        tv, tpos = lax.top_k(cand, PEER_TOPK)
        eidx = jnp.take_along_axis(cidx, tpos, axis=-1)
        gate = jax.nn.softmax(tv, axis=-1).astype(xc.dtype)
        a = jnp.einsum('td,thkd->thk', xc, u[eidx])
        hid = jax.nn.gelu(a, approximate=False) * gate
        return jnp.einsum('thk,thkd->td', hid, v[eidx])

    return lax.map(one_chunk, hc).reshape(B, S, D)


def setup_inputs(seed: int = 0) -> dict:
    key = jax.random.key(seed)
    ks = jax.random.split(key, 20)
    f32 = jnp.float32
    nrm = lambda k, shape, s: jax.random.normal(k, shape, f32) * s
    gain = lambda k, shape: 1.0 + 0.05 * jax.random.normal(k, shape, f32)
    return {
        'x': nrm(ks[0], (BATCH, SEQ, D_MODEL), 1.0),
        'p': nrm(ks[1], (DEPTH, BATCH, SEQ, PLE_DIM), 1.0),
        'g_mix': gain(ks[2], (DEPTH, D_MODEL)),
        'w_in': nrm(ks[3], (DEPTH, D_MODEL, W_IN), D_MODEL ** -0.5),
        'g_q': gain(ks[4], (DEPTH, HEAD_DIM)),
        'g_k': gain(ks[5], (DEPTH, HEAD_DIM)),
        'conv_w': nrm(ks[6], (DEPTH, CONV_K, CONV_W), CONV_K ** -0.5),
        'w_attn_out': nrm(ks[7], (DEPTH, ATTN_W, D_MODEL), ATTN_W ** -0.5),
        'w_conv_out': nrm(ks[8], (DEPTH, CONV_W, D_MODEL), CONV_W ** -0.5),
        'w_out': nrm(ks[9], (DEPTH, D_MODEL, D_MODEL), D_MODEL ** -0.5),
        'g_ffn': gain(ks[10], (DEPTH, D_MODEL)),
        'w_peer_q': nrm(ks[11], (DEPTH, D_MODEL, PEER_HEADS * PEER_DK), D_MODEL ** -0.5),
        'peer_sub_keys': nrm(ks[12], (DEPTH, PEER_HEADS, 2, PEER_N_KEYS, PEER_HALF), PEER_HALF ** -0.5),
        'peer_u': nrm(ks[13], (DEPTH, PEER_N_EXPERTS, D_MODEL), D_MODEL ** -0.5),
        'peer_v': nrm(ks[14], (DEPTH, PEER_N_EXPERTS, D_MODEL), PEER_HEADS ** -0.5),
        'g_ple': gain(ks[15], (DEPTH, D_MODEL)),
        'w_ple': nrm(ks[16], (DEPTH, PLE_DIM, D_MODEL), PLE_DIM ** -0.5),
        'w_ple_gate': nrm(ks[17], (DEPTH, D_MODEL, D_MODEL), D_MODEL ** -0.5),
    }


def reference(x, p, g_mix, w_in, g_q, g_k, conv_w, w_attn_out, w_conv_out, w_out,
              g_ffn, w_peer_q, peer_sub_keys, peer_u, peer_v, g_ple, w_ple, w_ple_gate):
    B, S, _ = x.shape
    cos, sin = axial_rope_tables(S)
    for i in range(DEPTH):
        h = rms_norm(x, g_mix[i])
        z = h @ w_in[i]
        q = z[..., OFF_Q:OFF_K].reshape(B, S, N_Q_HEADS, HEAD_DIM)
        k = z[..., OFF_K:OFF_V].reshape(B, S, N_KV_HEADS, HEAD_DIM)
        v = z[..., OFF_V:OFF_CH].reshape(B, S, N_KV_HEADS, HEAD_DIM)
        c_h = z[..., OFF_CH:OFF_CB]
        c_b = z[..., OFF_CB:OFF_CC]
        c_c = z[..., OFF_CC:OFF_GA]
        gate_a = z[..., OFF_GA:OFF_GC]
        gate_c = z[..., OFF_GC:W_IN]
        q = apply_rope(rms_norm(q, g_q[i]), cos, sin)
        k = apply_rope(rms_norm(k, g_k[i]), cos, sin)
        y_attn = block_attention(q, k, v) @ w_attn_out[i]
        y_conv = (c_b * short_conv(c_c * c_h, conv_w[i])) @ w_conv_out[i]
        merged = jax.nn.sigmoid(gate_a) * y_attn + jax.nn.sigmoid(gate_c) * y_conv
        x = x + merged @ w_out[i]
        x = x + peer(rms_norm(x, g_ffn[i]), w_peer_q[i], peer_sub_keys[i], peer_u[i], peer_v[i])
        x = x + (p[i] @ w_ple[i]) * jax.nn.sigmoid(rms_norm(x, g_ple[i]) @ w_ple_gate[i])
    return x
```

```python
import functools

import jax
import jax.numpy as jnp
from jax import lax
from jax.experimental import pallas as pl
from jax.experimental.pallas import tpu as pltpu

F32 = jnp.float32
BF16 = jnp.bfloat16

EPS = 1e-6
HEAD_DIM = 128
N_Q_HEADS = 8
N_KV_HEADS = 2
Q_GROUP = N_Q_HEADS // N_KV_HEADS
ROPE_AXIS_DIM = HEAD_DIM // 2
ROPE_THETA = 10000.0
GRID_W = 64
PEER_HEADS = 8
PEER_N_KEYS = 128
PEER_HALF = 128
PEER_TOPK = 16
PEER_SLOTS = PEER_HEADS * PEER_TOPK

LANES = 128
SUBLANES = 8
VMEM_LIMIT = 56 * 1024 * 1024


def _params(*sem):
    return pltpu.CompilerParams(dimension_semantics=sem, vmem_limit_bytes=VMEM_LIMIT)


def _rms(x, g):
    ms = jnp.mean(x * x, axis=-1, keepdims=True)
    return x * lax.rsqrt(ms + EPS) * g


def _norm_matmul_kernel(x_ref, g_ref, w_ref, z_ref, h_ref):
    @pl.when(pl.program_id(1) == 0)
    def _():
        h_ref[...] = _rms(x_ref[...], g_ref[...]).astype(BF16)

    z_ref[...] = jnp.dot(h_ref[...], w_ref[...], preferred_element_type=F32).astype(z_ref.dtype)


def _norm_matmul(x, g, w, *, tm, tn, out_dtype):
    t, d = x.shape
    n = w.shape[1]
    return pl.pallas_call(
        _norm_matmul_kernel,
        grid=(t // tm, n // tn),
        in_specs=[
            pl.BlockSpec((tm, d), lambda i, j: (i, 0)),
            pl.BlockSpec((1, d), lambda i, j: (0, 0)),
            pl.BlockSpec((d, tn), lambda i, j: (0, j)),
        ],
        out_specs=pl.BlockSpec((tm, tn), lambda i, j: (i, j)),
        out_shape=jax.ShapeDtypeStruct((t, n), out_dtype),
        scratch_shapes=[pltpu.VMEM((tm, d), BF16)],
        compiler_params=_params("parallel", "arbitrary"),
        name="in_proj",
    )(x, g, w)


def _qk_prep_kernel(q_ref, k_ref, cos_ref, sin_ref, gq_ref, gk_ref, qo_ref, ko_ref):
    cos = cos_ref[...]
    sin = sin_ref[...]
    lane = lax.broadcasted_iota(jnp.int32, cos.shape, 1)
    first = (lane % ROPE_AXIS_DIM) < (ROPE_AXIS_DIM // 2)

    def prep(xh, g, scale):
        y = _rms(xh.astype(F32), g)
        rot = jnp.where(first, -pltpu.roll(y, HEAD_DIM - 32, 1), pltpu.roll(y, 32, 1))
        return (y * cos + rot * sin) * scale

    scale = HEAD_DIM ** -0.5
    for h in range(N_Q_HEADS):
        sl = slice(h * HEAD_DIM, (h + 1) * HEAD_DIM)
        qo_ref[:, sl] = prep(q_ref[:, sl], gq_ref[...], scale).astype(qo_ref.dtype)
    for h in range(N_KV_HEADS):
        sl = slice(h * HEAD_DIM, (h + 1) * HEAD_DIM)
        ko_ref[:, sl] = prep(k_ref[:, sl], gk_ref[...], 1.0).astype(ko_ref.dtype)


def _qk_prep(z, cos, sin, gq, gk, *, seq, tm):
    t = z.shape[0]
    qw = N_Q_HEADS * HEAD_DIM
    kw = N_KV_HEADS * HEAD_DIM
    nseq = seq // tm
    return pl.pallas_call(
        _qk_prep_kernel,
        grid=(t // tm,),
        in_specs=[
            pl.BlockSpec((tm, qw), lambda i: (i, 0)),
            pl.BlockSpec((tm, kw), lambda i: (i, qw // kw)),
            pl.BlockSpec((tm, HEAD_DIM), lambda i: (i % nseq, 0)),
            pl.BlockSpec((tm, HEAD_DIM), lambda i: (i % nseq, 0)),
            pl.BlockSpec((1, HEAD_DIM), lambda i: (0, 0)),
            pl.BlockSpec((1, HEAD_DIM), lambda i: (0, 0)),
        ],
        out_specs=[
            pl.BlockSpec((tm, qw), lambda i: (i, 0)),
            pl.BlockSpec((tm, kw), lambda i: (i, 0)),
        ],
        out_shape=[
            jax.ShapeDtypeStruct((t, qw), BF16),
            jax.ShapeDtypeStruct((t, kw), BF16),
        ],
        compiler_params=_params("parallel"),
        name="qk_prep",
    )(z, z, cos, sin, gq, gk)


def _attn_kernel(q_ref, k_ref, v_ref, o_ref, m_ref, l_ref, acc_ref, *, tk):
    tq = q_ref.shape[0]
    seq = k_ref.shape[0]
    q = jnp.concatenate(
        [q_ref[:, g * HEAD_DIM:(g + 1) * HEAD_DIM] for g in range(Q_GROUP)], axis=0)
    m_ref[...] = jnp.full(m_ref.shape, -jnp.inf, F32)
    l_ref[...] = jnp.zeros(l_ref.shape, F32)
    acc_ref[...] = jnp.zeros(acc_ref.shape, F32)

    def chunk(c, carry):
        off = pl.multiple_of(c * tk, tk)
        k = k_ref[pl.ds(off, tk), :]
        v = v_ref[pl.ds(off, tk), :]
        s = lax.dot_general(q, k, (((1,), (1,)), ((), ())), preferred_element_type=F32)
        m_old = m_ref[...]
        m_new = jnp.maximum(m_old, jnp.max(s, axis=-1, keepdims=True))
        alpha = jnp.exp(m_old - m_new)
        p = jnp.exp(s - m_new)
        l_ref[...] = alpha * l_ref[...] + jnp.sum(p, axis=-1, keepdims=True)
        acc_ref[...] = alpha * acc_ref[...] + jnp.dot(
            p.astype(v.dtype), v, preferred_element_type=F32)
        m_ref[...] = m_new
        return carry

    lax.fori_loop(0, seq // tk, chunk, 0)
    o = acc_ref[...] / l_ref[...]
    for g in range(Q_GROUP):
        o_ref[:, g * HEAD_DIM:(g + 1) * HEAD_DIM] = o[g * tq:(g + 1) * tq, :].astype(o_ref.dtype)


def _attention(qn, kn, z, *, batch, seq, v_col, tq, tk):
    t = qn.shape[0]
    gw = Q_GROUP * HEAD_DIM
    nq = seq // tq
    return pl.pallas_call(
        functools.partial(_attn_kernel, tk=tk),
        grid=(batch, N_KV_HEADS, nq),
        in_specs=[
            pl.BlockSpec((tq, gw), lambda b, h, i: (b * nq + i, h)),
            pl.BlockSpec((seq, HEAD_DIM), lambda b, h, i: (b, h)),
            pl.BlockSpec((seq, HEAD_DIM), lambda b, h, i: (b, v_col // HEAD_DIM + h)),
        ],
        out_specs=pl.BlockSpec((tq, gw), lambda b, h, i: (b * nq + i, h)),
        out_shape=jax.ShapeDtypeStruct((t, N_Q_HEADS * HEAD_DIM), BF16),
        scratch_shapes=[
            pltpu.VMEM((Q_GROUP * tq, 1), F32),
            pltpu.VMEM((Q_GROUP * tq, 1), F32),
            pltpu.VMEM((Q_GROUP * tq, HEAD_DIM), F32),
        ],
        compiler_params=_params("parallel", "parallel", "arbitrary"),
        name="attention",
    )(qn, kn, z)


def _conv_kernel(ch_ref, cb_ref, cc_ref, w_ref, o_ref):
    u = cc_ref[...].astype(F32) * ch_ref[...].astype(F32)
    n = u.shape[0]
    row = lax.broadcasted_iota(jnp.int32, u.shape, 0)
    prev = jnp.where(row == 0, 0.0, pltpu.roll(u, 1, 0))
    nxt = jnp.where(row == n - 1, 0.0, pltpu.roll(u, n - 1, 0))
    w = w_ref[...]
    y = prev * w[0:1, :] + u * w[1:2, :] + nxt * w[2:3, :]
    o_ref[...] = (cb_ref[...].astype(F32) * y).astype(o_ref.dtype)


def _conv(z, conv_w, *, batch, seq, ch_col, cb_col, cc_col, tc):
    t = z.shape[0]
    cw = conv_w.shape[1]
    return pl.pallas_call(
        _conv_kernel,
        grid=(batch, cw // tc),
        in_specs=[
            pl.BlockSpec((seq, tc), lambda b, j: (b, ch_col // tc + j)),
            pl.BlockSpec((seq, tc), lambda b, j: (b, cb_col // tc + j)),
            pl.BlockSpec((seq, tc), lambda b, j: (b, cc_col // tc + j)),
            pl.BlockSpec((conv_w.shape[0], tc), lambda b, j: (0, j)),
        ],
        out_specs=pl.BlockSpec((seq, tc), lambda b, j: (b, j)),
        out_shape=jax.ShapeDtypeStruct((t, cw), BF16),
        compiler_params=_params("parallel", "parallel"),
        name="conv",
    )(z, z, z, conv_w)


def _merge_kernel(a_ref, c_ref, ga_ref, gc_ref, wa_ref, wc_ref, o_ref):
    ya = jnp.dot(a_ref[...], wa_ref[...], preferred_element_type=F32)
    yc = jnp.dot(c_ref[...], wc_ref[...], preferred_element_type=F32)
    ga = jax.nn.sigmoid(ga_ref[...].astype(F32))
    gc = jax.nn.sigmoid(gc_ref[...].astype(F32))
    o_ref[...] = (ga * ya + gc * yc).astype(o_ref.dtype)


def _merge(attn, conv, z, wa, wc, *, ga_col, gc_col, tm, tn):
    t, ka = attn.shape
    kc = conv.shape[1]
    n = wa.shape[1]
    return pl.pallas_call(
        _merge_kernel,
        grid=(t // tm, n // tn),
        in_specs=[
            pl.BlockSpec((tm, ka), lambda i, j: (i, 0)),
            pl.BlockSpec((tm, kc), lambda i, j: (i, 0)),
            pl.BlockSpec((tm, tn), lambda i, j: (i, ga_col // tn + j)),
            pl.BlockSpec((tm, tn), lambda i, j: (i, gc_col // tn + j)),
            pl.BlockSpec((ka, tn), lambda i, j: (0, j)),
            pl.BlockSpec((kc, tn), lambda i, j: (0, j)),
        ],
        out_specs=pl.BlockSpec((tm, tn), lambda i, j: (i, j)),
        out_shape=jax.ShapeDtypeStruct((t, n), BF16),
        compiler_params=_params("parallel", "arbitrary"),
        name="merge",
    )(attn, conv, z, z, wa, wc)


def _residual_matmul_kernel(x_ref, a_ref, w_ref, o_ref):
    o_ref[...] = x_ref[...] + jnp.dot(a_ref[...], w_ref[...], preferred_element_type=F32)


def _residual_matmul(x, a, w, *, tm, tn):
    t, k = a.shape
    n = w.shape[1]
    return pl.pallas_call(
        _residual_matmul_kernel,
        grid=(t // tm, n // tn),
        in_specs=[
            pl.BlockSpec((tm, tn), lambda i, j: (i, j)),
            pl.BlockSpec((tm, k), lambda i, j: (i, 0)),
            pl.BlockSpec((k, tn), lambda i, j: (0, j)),
        ],
        out_specs=pl.BlockSpec((tm, tn), lambda i, j: (i, j)),
        out_shape=jax.ShapeDtypeStruct((t, n), F32),
        compiler_params=_params("parallel", "arbitrary"),
        name="out_proj",
    )(x, a, w)


def _topk_rows(s, k, payload=None):
    n = s.shape[0]
    row = lax.broadcasted_iota(jnp.int32, s.shape, 0)
    vals, sel = [], []
    for _ in range(k):
        m = jnp.max(s, axis=0, keepdims=True)
        i = jnp.min(jnp.where(s == m, row, n), axis=0, keepdims=True)
        hit = row == i
        vals.append(m)
        if payload is None:
            sel.append(i)
        else:
            sel.append(jnp.max(jnp.where(hit, payload, -1), axis=0, keepdims=True))
        s = jnp.where(hit, -jnp.inf, s)
    return jnp.concatenate(vals, axis=0), jnp.concatenate(sel, axis=0)


def _peer_route_kernel(x_ref, g_ref, wqt_ref, sk_ref, h_ref, idx_ref, gate_ref, qt_ref):
    h = _rms(x_ref[...], g_ref[...])
    h_ref[...] = h
    qt_ref[...] = lax.dot_general(
        wqt_ref[...], h.astype(BF16), (((1,), (1,)), ((), ())),
        preferred_element_type=F32).astype(BF16)

    def head(hd, carry):
        top = []
        for half in range(2):
            off = pl.multiple_of((hd * 2 + half) * PEER_HALF, PEER_HALF)
            keys = sk_ref[pl.ds(off, PEER_N_KEYS), :]
            s = jnp.dot(keys, qt_ref[pl.ds(off, PEER_HALF), :],
                        preferred_element_type=F32)
            top.append(_topk_rows(s, PEER_TOPK))
        (sv0, si0), (sv1, si1) = top
        cand = jnp.concatenate([sv0[i:i + 1, :] + sv1 for i in range(PEER_TOPK)], axis=0)
        cidx = jnp.concatenate(
            [si0[i:i + 1, :] * PEER_N_KEYS + si1 for i in range(PEER_TOPK)], axis=0)
        tv, eidx = _topk_rows(cand, PEER_TOPK, payload=cidx)
        e = jnp.exp(tv - jnp.max(tv, axis=0, keepdims=True))
        gate = e / jnp.sum(e, axis=0, keepdims=True)
        o = pl.multiple_of(hd * PEER_TOPK, PEER_TOPK)
        idx_ref[pl.ds(o, PEER_TOPK), :] = eidx
        gate_ref[pl.ds(o, PEER_TOPK), :] = gate
        return carry

    lax.fori_loop(0, PEER_HEADS, head, 0)


def _peer_route(x1, g, wqt, sk, *, tm):
    t, d = x1.shape
    nq = wqt.shape[0]
    return pl.pallas_call(
        _peer_route_kernel,
        grid=(t // tm,),
        in_specs=[
            pl.BlockSpec((tm, d), lambda i: (i, 0)),
            pl.BlockSpec((1, d), lambda i: (0, 0)),
            pl.BlockSpec((nq, d), lambda i: (0, 0)),
            pl.BlockSpec(sk.shape, lambda i: (0, 0)),
        ],
        out_specs=[
            pl.BlockSpec((tm, d), lambda i: (i, 0)),
            pl.BlockSpec((PEER_SLOTS, tm), lambda i: (0, i)),
            pl.BlockSpec((PEER_SLOTS, tm), lambda i: (0, i)),
        ],
        out_shape=[
            jax.ShapeDtypeStruct((t, d), F32),
            jax.ShapeDtypeStruct((PEER_SLOTS, t), jnp.int32),
            jax.ShapeDtypeStruct((PEER_SLOTS, t), F32),
        ],
        scratch_shapes=[pltpu.VMEM((nq, tm), BF16)],
        compiler_params=_params("parallel"),
        name="peer_route",
    )(x1, g, wqt, sk)


def _peer_gather_kernel(idx_hbm, gate_ref, h_ref, x_ref, uv_hbm, o_ref,
                        idx_smem, buf, hid_ref, sem, isem, *, tb, nslot):
    step = pl.program_id(0)
    rows = buf.shape[2]
    ur = rows // 2

    nidx = tb * PEER_SLOTS
    icp = pltpu.make_async_copy(
        idx_hbm.at[pl.ds(pl.multiple_of(step * nidx, nidx), nidx)], idx_smem, isem)
    icp.start()
    icp.wait()

    def issue(tok, slot):
        for k in range(PEER_SLOTS):
            e = idx_smem[tok * PEER_SLOTS + k]
            pltpu.make_async_copy(uv_hbm.at[e], buf.at[slot, k], sem.at[slot]).start()

    def wait(slot):
        pltpu.make_async_copy(
            uv_hbm.at[pl.ds(0, PEER_SLOTS)], buf.at[slot], sem.at[slot]).wait()

    for s in range(nslot - 1):
        issue(s, s)

    ones_grp = (lax.broadcasted_iota(jnp.int32, (PEER_SLOTS, PEER_SLOTS * SUBLANES), 1) // SUBLANES
                == lax.broadcasted_iota(jnp.int32, (PEER_SLOTS, PEER_SLOTS * SUBLANES), 0)
                ).astype(BF16)
    tok_row = lax.broadcasted_iota(jnp.int32, (tb, LANES), 0)

    def body(tok, carry):
        slot = tok % nslot
        nxt = tok + nslot - 1

        @pl.when(nxt < tb)
        def _():
            issue(nxt, nxt % nslot)

        wait(slot)
        x = h_ref[tok]
        p = None
        for r in range(ur // SUBLANES):
            sl = slice(r * SUBLANES, (r + 1) * SUBLANES)
            term = buf[slot, :, sl, :] * x[sl, :]
            p = term if p is None else p + term
        p = p.reshape(PEER_SLOTS * SUBLANES, LANES).astype(BF16)
        gp = jnp.dot(ones_grp, p, preferred_element_type=F32)
        a = jnp.sum(gp, axis=1, keepdims=True)
        onehot = (tok_row == tok).astype(F32)
        gb = jnp.dot(gate_ref[...], onehot, preferred_element_type=F32,
                     precision=lax.Precision.HIGHEST)
        hid_ref[...] = (0.5 * a * (1.0 + lax.erf(a * (2.0 ** -0.5)))) * gb
        nacc = 4
        acc = [[None] * (ur // SUBLANES) for _ in range(nacc)]
        for k in range(PEER_SLOTS):
            hb = jnp.broadcast_to(hid_ref[k:k + 1, :], (SUBLANES, LANES))
            for r in range(ur // SUBLANES):
                sl = slice(ur + r * SUBLANES, ur + (r + 1) * SUBLANES)
                term = hb * buf[slot, k, sl, :]
                cur = acc[k % nacc][r]
                acc[k % nacc][r] = term if cur is None else cur + term
        outs = []
        for r in range(ur // SUBLANES):
            outs.append((acc[0][r] + acc[1][r]) + (acc[2][r] + acc[3][r]))
        o_ref[tok] = x_ref[tok] + jnp.concatenate(outs, axis=0)
        return carry

    lax.fori_loop(0, tb, body, 0)


def _peer_gather(idx_blk, gate_t, h2, x1, uv, *, tb, nslot):
    t, d = x1.shape
    rows = uv.shape[1]
    ur = rows // 2
    h3 = h2.reshape(t, ur, LANES)
    x3 = x1.reshape(t, ur, LANES)
    out = pl.pallas_call(
        functools.partial(_peer_gather_kernel, tb=tb, nslot=nslot),
        grid=(t // tb,),
        in_specs=[
            pl.BlockSpec(memory_space=pl.ANY),
            pl.BlockSpec((PEER_SLOTS, tb), lambda i: (0, i)),
            pl.BlockSpec((tb, ur, LANES), lambda i: (i, 0, 0)),
            pl.BlockSpec((tb, ur, LANES), lambda i: (i, 0, 0)),
            pl.BlockSpec(memory_space=pl.ANY),
        ],
        out_specs=pl.BlockSpec((tb, ur, LANES), lambda i: (i, 0, 0)),
        out_shape=jax.ShapeDtypeStruct((t, ur, LANES), F32),
        scratch_shapes=[
            pltpu.SMEM((tb * PEER_SLOTS,), jnp.int32),
            pltpu.VMEM((nslot, PEER_SLOTS, rows, LANES), F32),
            pltpu.VMEM((PEER_SLOTS, LANES), F32),
            pltpu.SemaphoreType.DMA((nslot,)),
            pltpu.SemaphoreType.DMA(()),
        ],
        compiler_params=_params("arbitrary"),
        name="peer_gather",
    )(idx_blk, gate_t, h3, x3, uv)
    return out.reshape(t, d)


def _ple_kernel(x_ref, xt_ref, g_ref, p_ref, wp_ref, wg_ref, o_ref, h_ref):
    @pl.when(pl.program_id(1) == 0)
    def _():
        h_ref[...] = _rms(x_ref[...], g_ref[...]).astype(BF16)

    e = jnp.dot(p_ref[...], wp_ref[...], preferred_element_type=F32)
    gt = jnp.dot(h_ref[...], wg_ref[...], preferred_element_type=F32)
    o_ref[...] = xt_ref[...] + e * jax.nn.sigmoid(gt)


def _ple(x2, g, p, wp, wg, *, tm, tn):
    t, d = x2.shape
    kp = p.shape[1]
    n = wg.shape[1]
    return pl.pallas_call(
        _ple_kernel,
        grid=(t // tm, n // tn),
        in_specs=[
            pl.BlockSpec((tm, d), lambda i, j: (i, 0)),
            pl.BlockSpec((tm, tn), lambda i, j: (i, j)),
            pl.BlockSpec((1, d), lambda i, j: (0, 0)),
            pl.BlockSpec((tm, kp), lambda i, j: (i, 0)),
            pl.BlockSpec((kp, tn), lambda i, j: (0, j)),
            pl.BlockSpec((d, tn), lambda i, j: (0, j)),
        ],
        out_specs=pl.BlockSpec((tm, tn), lambda i, j: (i, j)),
        out_shape=jax.ShapeDtypeStruct((t, n), F32),
        scratch_shapes=[pltpu.VMEM((tm, d), BF16)],
        compiler_params=_params("parallel", "arbitrary"),
        name="ple",
    )(x2, x2, g, p, wp, wg)


def _rope_tables(seq):
    rows = seq // GRID_W
    row = jnp.repeat(jnp.arange(rows, dtype=jnp.int32), GRID_W, total_repeat_length=seq)
    col = jnp.tile(jnp.arange(GRID_W, dtype=jnp.int32), rows)
    inv = ROPE_THETA ** (-jnp.arange(0, ROPE_AXIS_DIM, 2, dtype=F32) / ROPE_AXIS_DIM)
    ang_r = row.astype(F32)[:, None] * inv[None, :]
    ang_c = col.astype(F32)[:, None] * inv[None, :]
    ang = jnp.concatenate([ang_r, ang_r, ang_c, ang_c], axis=-1)
    return jnp.cos(ang), jnp.sin(ang)


def _layer(x, p, g_mix, w_in, g_q, g_k, conv_w, w_attn_out, w_conv_out, w_out, g_ffn,
           w_peer_q, peer_sub_keys, peer_u, peer_v, g_ple, w_ple, w_ple_gate, cos, sin,
           *, batch, seq, tiles):
    t, d = x.shape
    attn_w = N_Q_HEADS * HEAD_DIM
    kv_w = N_KV_HEADS * HEAD_DIM
    conv_cw = conv_w.shape[1]
    off_v = attn_w + kv_w
    off_ch = off_v + kv_w
    off_cb = off_ch + conv_cw
    off_cc = off_cb + conv_cw
    off_ga = off_cc + conv_cw
    off_gc = off_ga + d

    z = _norm_matmul(x, g_mix[None, :], w_in.astype(BF16),
                     tm=tiles["in_tm"], tn=tiles["in_tn"], out_dtype=BF16)
    qn, kn = _qk_prep(z, cos, sin, g_q[None, :], g_k[None, :], seq=seq, tm=tiles["qk_tm"])
    attn = _attention(qn, kn, z, batch=batch, seq=seq, v_col=off_v,
                      tq=tiles["tq"], tk=tiles["tk"])
    conv = _conv(z, conv_w, batch=batch, seq=seq, ch_col=off_ch, cb_col=off_cb,
                 cc_col=off_cc, tc=tiles["conv_tc"])
    merged = _merge(attn, conv, z, w_attn_out.astype(BF16), w_conv_out.astype(BF16),
                    ga_col=off_ga, gc_col=off_gc, tm=tiles["mm_tm"], tn=tiles["merge_tn"])
    x1 = _residual_matmul(x, merged, w_out.astype(BF16), tm=tiles["mm_tm"], tn=tiles["mm_tn"])

    nkeys = peer_sub_keys.shape[2]
    sk = peer_sub_keys.reshape(-1, peer_sub_keys.shape[-1]).astype(BF16)
    h2, idx_t, gate_t = _peer_route(x1, g_ffn[None, :], w_peer_q.T.astype(BF16), sk,
                                    tm=tiles["route_tm"])
    del nkeys
    tb = tiles["gather_tb"]
    idx_blk = idx_t.T.reshape(t * PEER_SLOTS)
    nexp = peer_u.shape[0]
    uv = jnp.concatenate(
        [peer_u.reshape(nexp, d // LANES, LANES), peer_v.reshape(nexp, d // LANES, LANES)], axis=1)
    x2 = _peer_gather(idx_blk, gate_t, h2, x1, uv, tb=tb, nslot=tiles["gather_nslot"])

    return _ple(x2, g_ple[None, :], p.astype(BF16), w_ple.astype(BF16), w_ple_gate.astype(BF16),
                tm=tiles["mm_tm"], tn=tiles["mm_tn"])


_TILES = dict(in_tm=1024, in_tn=512, qk_tm=512, tq=256, tk=512, conv_tc=256,
              mm_tm=512, merge_tn=512, mm_tn=512, route_tm=256, gather_tb=128, gather_nslot=4)


def _forward(x, p, g_mix, w_in, g_q, g_k, conv_w, w_attn_out, w_conv_out, w_out, g_ffn,
             w_peer_q, peer_sub_keys, peer_u, peer_v, g_ple, w_ple, w_ple_gate, *, tiles):
    batch, seq, d = x.shape
    depth = w_in.shape[0]
    cos, sin = _rope_tables(seq)
    xf = x.reshape(batch * seq, d)
    for i in range(depth):
        xf = _layer(xf, p[i].reshape(batch * seq, -1), g_mix[i], w_in[i], g_q[i], g_k[i],
                    conv_w[i], w_attn_out[i], w_conv_out[i], w_out[i], g_ffn[i], w_peer_q[i],
                    peer_sub_keys[i], peer_u[i], peer_v[i], g_ple[i], w_ple[i], w_ple_gate[i],
                    cos, sin, batch=batch, seq=seq, tiles=tiles)
    return xf.reshape(batch, seq, d)


def kernel(x, p, g_mix, w_in, g_q, g_k, conv_w, w_attn_out, w_conv_out, w_out, g_ffn,
           w_peer_q, peer_sub_keys, peer_u, peer_v, g_ple, w_ple, w_ple_gate):
    return _forward(x, p, g_mix, w_in, g_q, g_k, conv_w, w_attn_out, w_conv_out, w_out, g_ffn,
                    w_peer_q, peer_sub_keys, peer_u, peer_v, g_ple, w_ple, w_ple_gate,
                    tiles=_TILES)
```

```python
import functools

import jax
import jax.numpy as jnp
from jax import lax
from jax.experimental import pallas as pl
from jax.experimental.pallas import tpu as pltpu

F32 = jnp.float32
BF16 = jnp.bfloat16

EPS = 1e-6
HEAD_DIM = 128
N_Q_HEADS = 8
N_KV_HEADS = 2
Q_GROUP = N_Q_HEADS // N_KV_HEADS
ROPE_AXIS_DIM = HEAD_DIM // 2
ROPE_THETA = 10000.0
GRID_W = 64
PEER_HEADS = 8
PEER_N_KEYS = 128
PEER_HALF = 128
PEER_TOPK = 16
PEER_SLOTS = PEER_HEADS * PEER_TOPK

LANES = 128
SUBLANES = 8
VMEM_LIMIT = 56 * 1024 * 1024


def _params(*sem):
    return pltpu.CompilerParams(dimension_semantics=sem, vmem_limit_bytes=VMEM_LIMIT)


def _rms(x, g):
    ms = jnp.mean(x * x, axis=-1, keepdims=True)
    return x * lax.rsqrt(ms + EPS) * g


def _norm_matmul_kernel(x_ref, g_ref, w_ref, z_ref, h_ref):
    @pl.when(pl.program_id(1) == 0)
    def _():
        h_ref[...] = _rms(x_ref[...], g_ref[...]).astype(BF16)

    z_ref[...] = jnp.dot(h_ref[...], w_ref[...], preferred_element_type=F32).astype(z_ref.dtype)


def _norm_matmul(x, g, w, *, tm, tn, out_dtype):
    t, d = x.shape
    n = w.shape[1]
    return pl.pallas_call(
        _norm_matmul_kernel,
        grid=(t // tm, n // tn),
        in_specs=[
            pl.BlockSpec((tm, d), lambda i, j: (i, 0)),
            pl.BlockSpec((1, d), lambda i, j: (0, 0)),
            pl.BlockSpec((d, tn), lambda i, j: (0, j)),
        ],
        out_specs=pl.BlockSpec((tm, tn), lambda i, j: (i, j)),
        out_shape=jax.ShapeDtypeStruct((t, n), out_dtype),
        scratch_shapes=[pltpu.VMEM((tm, d), BF16)],
        compiler_params=_params("parallel", "arbitrary"),
        name="in_proj",
    )(x, g, w)


def _qk_prep_kernel(q_ref, k_ref, cos_ref, sin_ref, gq_ref, gk_ref, qo_ref, ko_ref):
    cos = cos_ref[...]
    sin = sin_ref[...]
    lane = lax.broadcasted_iota(jnp.int32, cos.shape, 1)
    first = (lane % ROPE_AXIS_DIM) < (ROPE_AXIS_DIM // 2)

    def prep(xh, g, scale):
        y = _rms(xh.astype(F32), g)
        rot = jnp.where(first, -pltpu.roll(y, HEAD_DIM - 32, 1), pltpu.roll(y, 32, 1))
        return (y * cos + rot * sin) * scale

    scale = HEAD_DIM ** -0.5 * 1.4426950408889634
    for h in range(N_Q_HEADS):
        sl = slice(h * HEAD_DIM, (h + 1) * HEAD_DIM)
        qo_ref[:, sl] = prep(q_ref[:, sl], gq_ref[...], scale).astype(qo_ref.dtype)
    for h in range(N_KV_HEADS):
        sl = slice(h * HEAD_DIM, (h + 1) * HEAD_DIM)
        ko_ref[:, sl] = prep(k_ref[:, sl], gk_ref[...], 1.0).astype(ko_ref.dtype)


def _qk_prep(z, cos, sin, gq, gk, *, seq, tm):
    t = z.shape[0]
    qw = N_Q_HEADS * HEAD_DIM
    kw = N_KV_HEADS * HEAD_DIM
    nseq = seq // tm
    return pl.pallas_call(
        _qk_prep_kernel,
        grid=(t // tm,),
        in_specs=[
            pl.BlockSpec((tm, qw), lambda i: (i, 0)),
            pl.BlockSpec((tm, kw), lambda i: (i, qw // kw)),
            pl.BlockSpec((tm, HEAD_DIM), lambda i: (i % nseq, 0)),
            pl.BlockSpec((tm, HEAD_DIM), lambda i: (i % nseq, 0)),
            pl.BlockSpec((1, HEAD_DIM), lambda i: (0, 0)),
            pl.BlockSpec((1, HEAD_DIM), lambda i: (0, 0)),
        ],
        out_specs=[
            pl.BlockSpec((tm, qw), lambda i: (i, 0)),
            pl.BlockSpec((tm, kw), lambda i: (i, 0)),
        ],
        out_shape=[
            jax.ShapeDtypeStruct((t, qw), BF16),
            jax.ShapeDtypeStruct((t, kw), BF16),
        ],
        compiler_params=_params("parallel"),
        name="qk_prep",
    )(z, z, cos, sin, gq, gk)


def _attn_kernel(q_ref, k_ref, v_ref, o_ref, qs_ref, os_ref, *, tk, tr):
    tq = q_ref.shape[0]
    seq = k_ref.shape[0]
    nlt = tk // LANES
    for g in range(Q_GROUP):
        qs_ref[g * tq:(g + 1) * tq, :] = q_ref[:, g * HEAD_DIM:(g + 1) * HEAD_DIM]

    def scores(q, c):
        off = pl.multiple_of(c * tk, tk)
        k = k_ref[pl.ds(off, tk), :]
        return lax.dot_general(q, k, (((1,), (1,)), ((), ())), preferred_element_type=F32)

    def row_block(rb, carry):
        r0 = pl.multiple_of(rb * tr, tr)
        q = qs_ref[pl.ds(r0, tr), :]

        def pass1(c, m):
            s = scores(q, c)
            for j in range(nlt):
                m = jnp.maximum(m, s[:, j * LANES:(j + 1) * LANES])
            return m

        m = lax.fori_loop(0, seq // tk, pass1, jnp.full((tr, LANES), -jnp.inf, F32),
                          unroll=True)
        mb = jnp.broadcast_to(jnp.max(m, axis=-1, keepdims=True), (tr, LANES))

        def pass2(c, lacc):
            l, acc = lacc
            s = scores(q, c)
            ps = []
            for j in range(nlt):
                pj = jnp.exp2(s[:, j * LANES:(j + 1) * LANES] - mb)
                l = l + pj
                ps.append(pj.astype(BF16))
            off = pl.multiple_of(c * tk, tk)
            v = v_ref[pl.ds(off, tk), :]
            acc = acc + jnp.dot(jnp.concatenate(ps, axis=1), v, preferred_element_type=F32)
            return l, acc

        l, acc = lax.fori_loop(
            0, seq // tk, pass2,
            (jnp.zeros((tr, LANES), F32), jnp.zeros((tr, HEAD_DIM), F32)), unroll=True)
        os_ref[pl.ds(r0, tr), :] = acc / jnp.sum(l, axis=-1, keepdims=True)
        return carry

    lax.fori_loop(0, Q_GROUP * tq // tr, row_block, 0)
    for g in range(Q_GROUP):
        o_ref[:, g * HEAD_DIM:(g + 1) * HEAD_DIM] = os_ref[g * tq:(g + 1) * tq, :].astype(o_ref.dtype)


def _attention(qn, kn, z, *, batch, seq, v_col, tq, tk, tr):
    t = qn.shape[0]
    gw = Q_GROUP * HEAD_DIM
    nq = seq // tq
    return pl.pallas_call(
        functools.partial(_attn_kernel, tk=tk, tr=tr),
        grid=(batch, N_KV_HEADS, nq),
        in_specs=[
            pl.BlockSpec((tq, gw), lambda b, h, i: (b * nq + i, h)),
            pl.BlockSpec((seq, HEAD_DIM), lambda b, h, i: (b, h)),
            pl.BlockSpec((seq, HEAD_DIM), lambda b, h, i: (b, v_col // HEAD_DIM + h)),
        ],
        out_specs=pl.BlockSpec((tq, gw), lambda b, h, i: (b * nq + i, h)),
        out_shape=jax.ShapeDtypeStruct((t, N_Q_HEADS * HEAD_DIM), BF16),
        scratch_shapes=[
            pltpu.VMEM((Q_GROUP * tq, HEAD_DIM), BF16),
            pltpu.VMEM((Q_GROUP * tq, HEAD_DIM), F32),
        ],
        compiler_params=_params("parallel", "parallel", "arbitrary"),
        name="attention",
    )(qn, kn, z)


def _conv_kernel(ch_ref, cb_ref, cc_ref, w_ref, o_ref):
    u = cc_ref[...].astype(F32) * ch_ref[...].astype(F32)
    n = u.shape[0]
    row = lax.broadcasted_iota(jnp.int32, u.shape, 0)
    prev = jnp.where(row == 0, 0.0, pltpu.roll(u, 1, 0))
    nxt = jnp.where(row == n - 1, 0.0, pltpu.roll(u, n - 1, 0))
    w = w_ref[...]
    y = prev * w[0:1, :] + u * w[1:2, :] + nxt * w[2:3, :]
    o_ref[...] = (cb_ref[...].astype(F32) * y).astype(o_ref.dtype)


def _conv(z, conv_w, *, batch, seq, ch_col, cb_col, cc_col, tc):
    t = z.shape[0]
    cw = conv_w.shape[1]
    return pl.pallas_call(
        _conv_kernel,
        grid=(batch, cw // tc),
        in_specs=[
            pl.BlockSpec((seq, tc), lambda b, j: (b, ch_col // tc + j)),
            pl.BlockSpec((seq, tc), lambda b, j: (b, cb_col // tc + j)),
            pl.BlockSpec((seq, tc), lambda b, j: (b, cc_col // tc + j)),
            pl.BlockSpec((conv_w.shape[0], tc), lambda b, j: (0, j)),
        ],
        out_specs=pl.BlockSpec((seq, tc), lambda b, j: (b, j)),
        out_shape=jax.ShapeDtypeStruct((t, cw), BF16),
        compiler_params=_params("parallel", "parallel"),
        name="conv",
    )(z, z, z, conv_w)


def _merge_kernel(a_ref, c_ref, ga_ref, gc_ref, wa_ref, wc_ref, o_ref):
    ya = jnp.dot(a_ref[...], wa_ref[...], preferred_element_type=F32)
    yc = jnp.dot(c_ref[...], wc_ref[...], preferred_element_type=F32)
    ga = jax.nn.sigmoid(ga_ref[...].astype(F32))
    gc = jax.nn.sigmoid(gc_ref[...].astype(F32))
    o_ref[...] = (ga * ya + gc * yc).astype(o_ref.dtype)


def _merge(attn, conv, z, wa, wc, *, ga_col, gc_col, tm, tn):
    t, ka = attn.shape
    kc = conv.shape[1]
    n = wa.shape[1]
    return pl.pallas_call(
        _merge_kernel,
        grid=(t // tm, n // tn),
        in_specs=[
            pl.BlockSpec((tm, ka), lambda i, j: (i, 0)),
            pl.BlockSpec((tm, kc), lambda i, j: (i, 0)),
            pl.BlockSpec((tm, tn), lambda i, j: (i, ga_col // tn + j)),
            pl.BlockSpec((tm, tn), lambda i, j: (i, gc_col // tn + j)),
            pl.BlockSpec((ka, tn), lambda i, j: (0, j)),
            pl.BlockSpec((kc, tn), lambda i, j: (0, j)),
        ],
        out_specs=pl.BlockSpec((tm, tn), lambda i, j: (i, j)),
        out_shape=jax.ShapeDtypeStruct((t, n), BF16),
        compiler_params=_params("parallel", "arbitrary"),
        name="merge",
    )(attn, conv, z, z, wa, wc)


def _residual_matmul_kernel(x_ref, a_ref, w_ref, o_ref):
    o_ref[...] = x_ref[...] + jnp.dot(a_ref[...], w_ref[...], preferred_element_type=F32)


def _residual_matmul(x, a, w, *, tm, tn):
    t, k = a.shape
    n = w.shape[1]
    return pl.pallas_call(
        _residual_matmul_kernel,
        grid=(t // tm, n // tn),
        in_specs=[
            pl.BlockSpec((tm, tn), lambda i, j: (i, j)),
            pl.BlockSpec((tm, k), lambda i, j: (i, 0)),
            pl.BlockSpec((k, tn), lambda i, j: (0, j)),
        ],
        out_specs=pl.BlockSpec((tm, tn), lambda i, j: (i, j)),
        out_shape=jax.ShapeDtypeStruct((t, n), F32),
        compiler_params=_params("parallel", "arbitrary"),
        name="out_proj",
    )(x, a, w)


def _topk_rows(s, k, payload=None):
    n = s.shape[0]
    row = lax.broadcasted_iota(jnp.int32, s.shape, 0)
    vals, sel = [], []
    for _ in range(k):
        m = jnp.max(s, axis=0, keepdims=True)
        i = jnp.min(jnp.where(s == m, row, n), axis=0, keepdims=True)
        hit = row == i
        vals.append(m)
        if payload is None:
            sel.append(i)
        else:
            sel.append(jnp.max(jnp.where(hit, payload, -1), axis=0, keepdims=True))
        s = jnp.where(hit, -jnp.inf, s)
    return jnp.concatenate(vals, axis=0), jnp.concatenate(sel, axis=0)


def _pair_candidates(sv0, si0, sv1, si1):
    assert PEER_TOPK == 2 * SUBLANES
    tm = sv0.shape[1]
    jrow = lax.broadcasted_iota(jnp.int32, (SUBLANES, tm), 0)
    vals = [sv0[0:1, :] + sv1]
    ids = [si0[0:1, :] * PEER_N_KEYS + si1]
    for i in range(1, SUBLANES):
        v = sv0[i:i + 1, :] + sv1[0:SUBLANES, :]
        cnt = PEER_TOPK // (i + 1)
        if cnt < SUBLANES:
            v = jnp.where(jrow < cnt, v, -jnp.inf)
        vals.append(v)
        ids.append(si0[i:i + 1, :] * PEER_N_KEYS + si1[0:SUBLANES, :])
    vals.append(sv0[SUBLANES:, :] + sv1[0:1, :])
    ids.append(si0[SUBLANES:, :] * PEER_N_KEYS + si1[0:1, :])
    return jnp.concatenate(vals, axis=0), jnp.concatenate(ids, axis=0)


def _peer_route_kernel(x_ref, g_ref, wqt_ref, sk_ref, h_ref, idx_ref, gate_ref, qt_ref):
    h = _rms(x_ref[...], g_ref[...])
    h_ref[...] = h
    qt_ref[...] = lax.dot_general(
        wqt_ref[...], h.astype(BF16), (((1,), (1,)), ((), ())),
        preferred_element_type=F32).astype(BF16)

    def head(hd, carry):
        top = []
        for half in range(2):
            off = pl.multiple_of((hd * 2 + half) * PEER_HALF, PEER_HALF)
            keys = sk_ref[pl.ds(off, PEER_N_KEYS), :]
            s = jnp.dot(keys, qt_ref[pl.ds(off, PEER_HALF), :],
                        preferred_element_type=F32)
            top.append(_topk_rows(s, PEER_TOPK))
        (sv0, si0), (sv1, si1) = top
        cand, cidx = _pair_candidates(sv0, si0, sv1, si1)
        tv, eidx = _topk_rows(cand, PEER_TOPK, payload=cidx)
        e = jnp.exp(tv - jnp.max(tv, axis=0, keepdims=True))
        gate = e / jnp.sum(e, axis=0, keepdims=True)
        o = pl.multiple_of(hd * PEER_TOPK, PEER_TOPK)
        idx_ref[pl.ds(o, PEER_TOPK), :] = eidx
        gate_ref[pl.ds(o, PEER_TOPK), :] = gate
        return carry

    lax.fori_loop(0, PEER_HEADS, head, 0)


def _peer_route(x1, g, wqt, sk, *, tm):
    t, d = x1.shape
    nq = wqt.shape[0]
    return pl.pallas_call(
        _peer_route_kernel,
        grid=(t // tm,),
        in_specs=[
            pl.BlockSpec((tm, d), lambda i: (i, 0)),
            pl.BlockSpec((1, d), lambda i: (0, 0)),
            pl.BlockSpec((nq, d), lambda i: (0, 0)),
            pl.BlockSpec(sk.shape, lambda i: (0, 0)),
        ],
        out_specs=[
            pl.BlockSpec((tm, d), lambda i: (i, 0)),
            pl.BlockSpec((PEER_SLOTS, tm), lambda i: (0, i)),
            pl.BlockSpec((PEER_SLOTS, tm), lambda i: (0, i)),
        ],
        out_shape=[
            jax.ShapeDtypeStruct((t, d), F32),
            jax.ShapeDtypeStruct((PEER_SLOTS, t), jnp.int32),
            jax.ShapeDtypeStruct((PEER_SLOTS, t), F32),
        ],
        scratch_shapes=[pltpu.VMEM((nq, tm), BF16)],
        compiler_params=_params("parallel"),
        name="peer_route",
    )(x1, g, wqt, sk)


def _peer_gather_kernel(idx_hbm, gate_ref, h_ref, x_ref, tab_hbm, o_ref,
                        idx_smem, buf, hb_ref, sem, isem, *, tb, nslot):
    step = pl.program_id(0)
    last = pl.num_programs(0) - 1
    nidx = (tb + nslot) * PEER_SLOTS
    base = pl.multiple_of(step * (tb * PEER_SLOTS), tb * PEER_SLOTS)
    icp = pltpu.make_async_copy(idx_hbm.at[pl.ds(base, nidx)], idx_smem, isem)
    icp.start()
    icp.wait()

    d = h_ref.shape[1]
    ngrp = PEER_SLOTS // SUBLANES
    nlt = d // LANES
    per = PEER_SLOTS // (2 * ngrp)

    def issue_part(tok, slot, part):
        for k in range(part * per, (part + 1) * per):
            e = idx_smem[tok * PEER_SLOTS + k]
            pltpu.make_async_copy(tab_hbm.at[e], buf.at[slot, pl.ds(k, 1), :],
                                  sem.at[slot]).start(priority=k % 2)

    def wait(slot):
        pltpu.make_async_copy(buf.at[slot], buf.at[slot], sem.at[slot]).wait()

    def words(slot, g):
        return buf[slot, g * SUBLANES:(g + 1) * SUBLANES, :]

    lane = lax.broadcasted_iota(jnp.int32, (PEER_SLOTS, tb), 1)

    def u_group(slot, g, xb):
        prod = pltpu.bitcast(words(slot, g) << 16, F32) * xb
        s = prod[:, 0:LANES]
        for c in range(1, nlt):
            s = s + prod[:, c * LANES:(c + 1) * LANES]
        return s

    def u_finish(tok, parts, par):
        a = jnp.sum(jnp.concatenate(parts, axis=0), axis=1, keepdims=True)
        gcol = jnp.sum(jnp.where(lane == tok, gate_ref[...], 0.0), axis=1, keepdims=True)
        hid = (0.5 * a * (1.0 + lax.erf(a * (2.0 ** -0.5)))) * gcol
        hb_ref[par] = jnp.broadcast_to(hid, (PEER_SLOTS, LANES))

    def u_only(tok, slot, par):
        wait(slot)
        xb = jnp.broadcast_to(h_ref[pl.ds(tok, 1), :], (SUBLANES, d))
        u_finish(tok, [u_group(slot, g, xb) for g in range(ngrp)], par)

    def token(tok, j, *, with_next=True):
        jn = (j + 1) % nslot
        jp = (j - 1) % nslot
        par = j % 2
        parts = []
        if with_next:
            wait(jn)
            xb = jnp.broadcast_to(h_ref[pl.ds(tok + 1, 1), :], (SUBLANES, d))
        for g in range(ngrp):
            if with_next:
                parts.append(u_group(jn, g, xb))
            issue_part(tok + nslot - 1, jp, g)
        acc = None
        for g in range(ngrp):
            v = pltpu.bitcast(words(j, g) & jnp.uint32(0xFFFF0000), F32)
            hbg = hb_ref[par, g * SUBLANES:(g + 1) * SUBLANES, :]
            term = jnp.concatenate(
                [v[:, c * LANES:(c + 1) * LANES] * hbg for c in range(nlt)], axis=1)
            acc = term if acc is None else acc + term
            issue_part(tok + nslot - 1, jp, ngrp + g)
        o_ref[pl.ds(tok, 1), :] = x_ref[pl.ds(tok, 1), :] + jnp.sum(acc, axis=0, keepdims=True)
        if with_next:
            u_finish(tok + 1, parts, 1 - par)

    @pl.when(step == 0)
    def _():
        for j in range(nslot - 1):
            for part in range(2 * ngrp):
                issue_part(j, j, part)

    u_only(0, 0, 0)

    def group(g, carry):
        for j in range(nslot):
            token(g * nslot + j, j)
        return carry

    ngroups = (tb - 1) // nslot
    lax.fori_loop(0, ngroups, group, 0)
    for tok in range(ngroups * nslot, tb - 1):
        token(tok, tok % nslot)
    token(tb - 1, (tb - 1) % nslot, with_next=False)

    @pl.when(step == last)
    def _():
        for j in range(nslot - 1):
            wait(j)


def _peer_gather(idx_pad, gate_t, h2, x1, tab, *, tb, nslot):
    t, d = x1.shape
    assert tb % nslot == 0 and nslot % 2 == 0 and t % tb == 0
    return pl.pallas_call(
        functools.partial(_peer_gather_kernel, tb=tb, nslot=nslot),
        grid=(t // tb,),
        in_specs=[
            pl.BlockSpec(memory_space=pl.ANY),
            pl.BlockSpec((PEER_SLOTS, tb), lambda i: (0, i)),
            pl.BlockSpec((tb, d), lambda i: (i, 0)),
            pl.BlockSpec((tb, d), lambda i: (i, 0)),
            pl.BlockSpec(memory_space=pl.ANY),
        ],
        out_specs=pl.BlockSpec((tb, d), lambda i: (i, 0)),
        out_shape=jax.ShapeDtypeStruct((t, d), F32),
        scratch_shapes=[
            pltpu.SMEM(((tb + nslot) * PEER_SLOTS,), jnp.int32),
            pltpu.VMEM((nslot, PEER_SLOTS, d), jnp.uint32),
            pltpu.VMEM((2, PEER_SLOTS, LANES), F32),
            pltpu.SemaphoreType.DMA((nslot,)),
            pltpu.SemaphoreType.DMA(()),
        ],
        compiler_params=_params("arbitrary"),
        name="peer_gather",
    )(idx_pad, gate_t, h2, x1, tab)


def _pack_expert_rows(u, v):
    ub = lax.bitcast_convert_type(u.astype(BF16), jnp.uint16).astype(jnp.uint32)
    vb = lax.bitcast_convert_type(v.astype(BF16), jnp.uint16).astype(jnp.uint32)
    return (ub | (vb << 16)).reshape(u.shape[0], 1, u.shape[1])


def _ple_kernel(x_ref, xt_ref, g_ref, p_ref, wp_ref, wg_ref, o_ref, h_ref):
    @pl.when(pl.program_id(1) == 0)
    def _():
        h_ref[...] = _rms(x_ref[...], g_ref[...]).astype(BF16)

    e = jnp.dot(p_ref[...], wp_ref[...], preferred_element_type=F32)
    gt = jnp.dot(h_ref[...], wg_ref[...], preferred_element_type=F32)
    o_ref[...] = xt_ref[...] + e * jax.nn.sigmoid(gt)


def _ple(x2, g, p, wp, wg, *, tm, tn):
    t, d = x2.shape
    kp = p.shape[1]
    n = wg.shape[1]
    return pl.pallas_call(
        _ple_kernel,
        grid=(t // tm, n // tn),
        in_specs=[
            pl.BlockSpec((tm, d), lambda i, j: (i, 0)),
            pl.BlockSpec((tm, tn), lambda i, j: (i, j)),
            pl.BlockSpec((1, d), lambda i, j: (0, 0)),
            pl.BlockSpec((tm, kp), lambda i, j: (i, 0)),
            pl.BlockSpec((kp, tn), lambda i, j: (0, j)),
            pl.BlockSpec((d, tn), lambda i, j: (0, j)),
        ],
        out_specs=pl.BlockSpec((tm, tn), lambda i, j: (i, j)),
        out_shape=jax.ShapeDtypeStruct((t, n), F32),
        scratch_shapes=[pltpu.VMEM((tm, d), BF16)],
        compiler_params=_params("parallel", "arbitrary"),
        name="ple",
    )(x2, x2, g, p, wp, wg)


def _rope_tables(seq):
    rows = seq // GRID_W
    row = jnp.repeat(jnp.arange(rows, dtype=jnp.int32), GRID_W, total_repeat_length=seq)
    col = jnp.tile(jnp.arange(GRID_W, dtype=jnp.int32), rows)
    inv = ROPE_THETA ** (-jnp.arange(0, ROPE_AXIS_DIM, 2, dtype=F32) / ROPE_AXIS_DIM)
    ang_r = row.astype(F32)[:, None] * inv[None, :]
    ang_c = col.astype(F32)[:, None] * inv[None, :]
    ang = jnp.concatenate([ang_r, ang_r, ang_c, ang_c], axis=-1)
    return jnp.cos(ang), jnp.sin(ang)


def _layer(x, p, g_mix, w_in, g_q, g_k, conv_w, w_attn_out, w_conv_out, w_out, g_ffn,
           w_peer_q, peer_sub_keys, peer_u, peer_v, g_ple, w_ple, w_ple_gate, cos, sin,
           *, batch, seq, tiles):
    t, d = x.shape
    attn_w = N_Q_HEADS * HEAD_DIM
    kv_w = N_KV_HEADS * HEAD_DIM
    conv_cw = conv_w.shape[1]
    off_v = attn_w + kv_w
    off_ch = off_v + kv_w
    off_cb = off_ch + conv_cw
    off_cc = off_cb + conv_cw
    off_ga = off_cc + conv_cw
    off_gc = off_ga + d

    z = _norm_matmul(x, g_mix[None, :], w_in.astype(BF16),
                     tm=tiles["in_tm"], tn=tiles["in_tn"], out_dtype=BF16)
    qn, kn = _qk_prep(z, cos, sin, g_q[None, :], g_k[None, :], seq=seq, tm=tiles["qk_tm"])
    attn = _attention(qn, kn, z, batch=batch, seq=seq, v_col=off_v,
                      tq=tiles["tq"], tk=tiles["tk"], tr=tiles["tr"])
    conv = _conv(z, conv_w, batch=batch, seq=seq, ch_col=off_ch, cb_col=off_cb,
                 cc_col=off_cc, tc=tiles["conv_tc"])
    merged = _merge(attn, conv, z, w_attn_out.astype(BF16), w_conv_out.astype(BF16),
                    ga_col=off_ga, gc_col=off_gc, tm=tiles["mm_tm"], tn=tiles["merge_tn"])
    x1 = _residual_matmul(x, merged, w_out.astype(BF16), tm=tiles["mm_tm"], tn=tiles["mm_tn"])

    sk =peer_sub_keys.reshape(-1, peer_sub_keys.shape[-1]).astype(BF16)
    h2, idx_t, gate_t = _peer_route(x1, g_ffn[None, :], w_peer_q.T.astype(BF16), sk,
                                    tm=tiles["route_tm"])
    tb = tiles["gather_tb"]
    nslot = tiles["gather_nslot"]
    idx_pad = jnp.concatenate([idx_t.T.reshape(t * PEER_SLOTS),
                               jnp.zeros((nslot * PEER_SLOTS,), jnp.int32)])
    x2 = _peer_gather(idx_pad, gate_t, h2, x1, _pack_expert_rows(peer_u, peer_v),
                      tb=tb, nslot=nslot)

    return _ple(x2, g_ple[None, :], p.astype(BF16), w_ple.astype(BF16), w_ple_gate.astype(BF16),
                tm=tiles["mm_tm"], tn=tiles["mm_tn"])


_TILES = dict(in_tm=1024, in_tn=512, qk_tm=512, tq=256, tk=512, tr=128, conv_tc=256,
              mm_tm=512, merge_tn=512, mm_tn=512, route_tm=256, gather_tb=128, gather_nslot=8)


def _forward(x, p, g_mix, w_in, g_q, g_k, conv_w, w_attn_out, w_conv_out, w_out, g_ffn,
             w_peer_q, peer_sub_keys, peer_u, peer_v, g_ple, w_ple, w_ple_gate, *, tiles):
    batch, seq, d = x.shape
    depth = w_in.shape[0]
    cos, sin = _rope_tables(seq)
    xf = x.reshape(batch * seq, d)
    for i in range(depth):
        xf = _layer(xf, p[i].reshape(batch * seq, -1), g_mix[i], w_in[i], g_q[i], g_k[i],
                    conv_w[i], w_attn_out[i], w_conv_out[i], w_out[i], g_ffn[i], w_peer_q[i],
                    peer_sub_keys[i], peer_u[i], peer_v[i], g_ple[i], w_ple[i], w_ple_gate[i],
                    cos, sin, batch=batch, seq=seq, tiles=tiles)
    return xf.reshape(batch, seq, d)


def kernel(x, p, g_mix, w_in, g_q, g_k, conv_w, w_attn_out, w_conv_out, w_out, g_ffn,
           w_peer_q, peer_sub_keys, peer_u, peer_v, g_ple, w_ple, w_ple_gate):
    return _forward(x, p, g_mix, w_in, g_q, g_k, conv_w, w_attn_out, w_conv_out, w_out, g_ffn,
                    w_peer_q, peer_sub_keys, peer_u, peer_v, g_ple, w_ple, w_ple_gate,
                    tiles=_TILES)
```
